```python
import jax, jax.numpy as jnp
from jax import lax
import numpy as np


D_MODEL = 1024
BATCH = 8
SEQ = 2048
DEPTH = 4
DEC_BATCH = 16
DEC_SEQ = 32
PAST_LEN = 2048

CHUNK = 64
D_RNN = D_MODEL
RNN_BLOCKS = 16
RNN_BW = D_RNN // RNN_BLOCKS
CONV_W = 4
RG_C = 8.0
HEAD_DIM = 64
N_Q_HEADS = D_MODEL // HEAD_DIM
N_KV_HEADS = 2
GQA = N_Q_HEADS // N_KV_HEADS
WINDOW = 128
WIN_CHUNKS = WINDOW // CHUNK
ROPE_THETA = 10000.0
N_EXPERTS = 64
TOP_K = 8
N_GROUPS = 8
TOPK_GROUPS = 4
D_EXPERT = 256
D_SHARED = 256
ROUTED_SCALE = 2.5
MOE_BLOCK = 128
DN_ALPHA = (2 * DEPTH) ** 0.25
DN_BETA = (8 * DEPTH) ** -0.25
LN_EPS = 1e-5
Q_OFF = D_RNN
K_OFF = Q_OFF + N_Q_HEADS * HEAD_DIM
V_OFF = K_OFF + N_KV_HEADS * HEAD_DIM
GR_OFF = V_OFF + N_KV_HEADS * HEAD_DIM
GA_OFF = GR_OFF + D_MODEL
IN_COLS = GA_OFF + D_MODEL

kernel_name = 'hawk_swa_moe_streaming_step'


def _layernorm(x, g, b):
    xf = x.astype(jnp.float32)
    mu = jnp.mean(xf, -1, keepdims=True)
    var = jnp.mean(jnp.square(xf - mu), -1, keepdims=True)
    return ((xf - mu) * lax.rsqrt(var + LN_EPS) * g.astype(jnp.float32) + b.astype(jnp.float32)).astype(x.dtype)


def _rope(t, pos):
    half = HEAD_DIM // 2
    inv = jnp.power(ROPE_THETA, -jnp.arange(half, dtype=jnp.float32) / half)
    ang = pos.astype(jnp.float32)[:, None] * inv[None, :]
    cos = jnp.cos(ang)[None, :, None, :]
    sin = jnp.sin(ang)[None, :, None, :]
    tf = t.astype(jnp.float32)
    t1, t2 = tf[..., :half], tf[..., half:]
    return jnp.concatenate([t1 * cos - t2 * sin, t2 * cos + t1 * sin], -1).astype(t.dtype)


def _lin_combine(left, right):
    a1, b1 = left
    a2, b2 = right
    return a1 * a2, a2 * b1 + b2


def _rglru(xr, conv_buf, h0, conv_w, conv_b, w_a, b_a, w_x, b_x, a_param, first_reset):
    B, T, _ = xr.shape
    xc = jnp.concatenate([conv_buf.astype(xr.dtype), xr], axis=1)
    u = sum((xc[:, j:j + T] * conv_w[j] for j in range(CONV_W)), conv_b)
    new_buf = xc[:, T:]
    ub = u.reshape(B, T, RNN_BLOCKS, RNN_BW)
    gx = jax.nn.sigmoid((jnp.einsum('btnc,ncd->btnd', ub, w_x).reshape(B, T, D_RNN) + b_x).astype(jnp.float32))
    ga = jax.nn.sigmoid((jnp.einsum('btnc,ncd->btnd', ub, w_a).reshape(B, T, D_RNN) + b_a).astype(jnp.float32))
    log_a = -RG_C * ga * jax.nn.softplus(-a_param.astype(jnp.float32))
    a = jnp.exp(log_a)
    mult = jnp.sqrt(-jnp.expm1(2.0 * log_a))
    if first_reset:
        mult = mult.at[:, 0].set(1.0)
    b = mult * gx * u.astype(jnp.float32)
    b = b.at[:, 0].add(a[:, 0] * h0.astype(jnp.float32))
    _, h = lax.associative_scan(_lin_combine, (a, b), axis=1)
    return h.astype(xr.dtype), h[:, -1].astype(h0.dtype), new_buf


def _sink_softmax(s, sinks):
    sk = jnp.broadcast_to(sinks.astype(jnp.float32).reshape(N_KV_HEADS, GQA, 1, 1), s.shape[:-1] + (1,))
    p = jax.nn.softmax(jnp.concatenate([s, sk], -1), axis=-1)
    return p[..., :-1]


def _attn_prompt(q, k, v, sinks):
    B, S = q.shape[:2]
    NC = S // CHUNK
    pad = WIN_CHUNKS * CHUNK
    KB = (WIN_CHUNKS + 1) * CHUNK
    qc = q.reshape(B, NC, CHUNK, N_KV_HEADS, GQA, HEAD_DIM)

    def band(t):
        tp = jnp.concatenate([jnp.zeros((B, pad) + t.shape[2:], t.dtype), t], 1)
        tp = tp.reshape(B, NC + WIN_CHUNKS, CHUNK, N_KV_HEADS, HEAD_DIM)
        return jnp.concatenate([tp[:, j:j + NC] for j in range(WIN_CHUNKS + 1)], axis=2)

    kb, vb = band(k), band(v)
    kpos = jnp.arange(NC)[:, None] * CHUNK + jnp.arange(KB)[None, :] - pad
    valid = (kpos >= 0)[None, :, None, None, None, :]
    s = jnp.einsum('bnqkgd,bnskd->bnkgqs', qc, kb).astype(jnp.float32) * (HEAD_DIM ** -0.5)
    s = jnp.where(valid, s, -jnp.inf)
    p = _sink_softmax(s, sinks).astype(v.dtype)
    o = jnp.einsum('bnkgqs,bnskd->bnqkgd', p, vb)
    return o.reshape(B, S, N_Q_HEADS * HEAD_DIM)


def _attn_sample(q, k_all, v_all, sinks):
    B, T = q.shape[:2]
    qg = q.reshape(B, T, N_KV_HEADS, GQA, HEAD_DIM)
    s = jnp.einsum('btkgd,bskd->bkgts', qg, k_all).astype(jnp.float32) * (HEAD_DIM ** -0.5)
    p = _sink_softmax(s, sinks).astype(v_all.dtype)
    o = jnp.einsum('bkgts,bskd->btkgd', p, v_all)
    return o.reshape(B, T, N_Q_HEADS * HEAD_DIM)


def _route(x2, w_router, e_bias):
    N = x2.shape[0]
    scores = jax.nn.sigmoid(jnp.dot(x2, w_router).astype(jnp.float32))
    biased = scores + e_bias.astype(jnp.float32)
    grp = biased.reshape(N, N_GROUPS, N_EXPERTS // N_GROUPS)
    grp_score = lax.top_k(grp, 2)[0].sum(-1)
    _, gidx = lax.top_k(grp_score, TOPK_GROUPS)
    gmask = jnp.any(gidx[..., None] == jnp.arange(N_GROUPS), axis=-2)
    emask = jnp.repeat(gmask, N_EXPERTS // N_GROUPS, axis=-1)
    _, eidx = lax.top_k(jnp.where(emask, biased, -jnp.inf), TOP_K)
    w = jnp.take_along_axis(scores, eidx, -1)
    gate = w / jnp.sum(w, -1, keepdims=True) * ROUTED_SCALE
    return eidx, gate


def _moe_routed(x2, eidx, gate, w1, w3, w2):
    N = x2.shape[0]
    NK = N * TOP_K
    e_flat = eidx.reshape(-1)
    tok_flat = jnp.repeat(jnp.arange(N, dtype=jnp.int32), TOP_K)
    g_flat = gate.reshape(-1)
    order = jnp.argsort(e_flat, stable=True)
    e_s, tok_s, g_s = e_flat[order], tok_flat[order], g_flat[order]
    counts = jnp.zeros((N_EXPERTS,), jnp.int32).at[e_flat].add(1)
    start = jnp.cumsum(counts) - counts
    pcounts = (counts + MOE_BLOCK - 1) // MOE_BLOCK * MOE_BLOCK
    pend = jnp.cumsum(pcounts)
    pstart = pend - pcounts
    dest = pstart[e_s] + (jnp.arange(NK, dtype=jnp.int32) - start[e_s])
    n_blocks = -(-NK // MOE_BLOCK) + N_EXPERTS
    P = n_blocks * MOE_BLOCK
    ptok = jnp.full((P,), N, jnp.int32).at[dest].set(tok_s)
    pgate = jnp.zeros((P,), jnp.float32).at[dest].set(g_s)
    blk_start = jnp.arange(n_blocks, dtype=jnp.int32) * MOE_BLOCK
    blk_e = jnp.minimum(jnp.sum(pend[None, :] <= blk_start[:, None], axis=1), N_EXPERTS - 1).astype(jnp.int32)
    xpad = jnp.concatenate([x2, jnp.zeros((1, x2.shape[1]), x2.dtype)], 0)

    def run_block(args):
        tok, e = args
        xb = xpad[tok]
        hdn = jax.nn.silu(xb @ w1[e]) * (xb @ w3[e])
        return hdn @ w2[e]

    yb = lax.map(run_block, (ptok.reshape(n_blocks, MOE_BLOCK), blk_e))
    yb = yb.reshape(P, -1).astype(jnp.float32) * pgate[:, None]
    return jax.ops.segment_sum(yb, ptok, num_segments=N + 1)[:N].astype(x2.dtype)


def _layer(x, pos, first_reset, conv_buf, h0, past_k, past_v, lw):
    (w_in, conv_w, conv_b, w_a, b_a, w_x, b_x, a_param, sinks, w_br_rnn, w_br_attn, w_out,
     ln1_g, ln1_b, w_router, e_bias, w1, w3, w2, ws1, ws3, ws2, ln2_g, ln2_b) = lw
    B, T, D = x.shape
    z = jnp.einsum('btd,dc->btc', x, w_in)
    xr = z[..., :Q_OFF]
    q = _rope(z[..., Q_OFF:K_OFF].reshape(B, T, N_Q_HEADS, HEAD_DIM), pos)
    k = _rope(z[..., K_OFF:V_OFF].reshape(B, T, N_KV_HEADS, HEAD_DIM), pos)
    v = z[..., V_OFF:GR_OFF].reshape(B, T, N_KV_HEADS, HEAD_DIM)
    g_rnn = jax.nn.sigmoid(z[..., GR_OFF:GA_OFF])
    g_attn = jax.nn.sigmoid(z[..., GA_OFF:])
    y_rnn, h_last, new_buf = _rglru(xr, conv_buf, h0, conv_w, conv_b, w_a, b_a, w_x, b_x, a_param, first_reset)
    if past_k is None:
        o = _attn_prompt(q, k, v, sinks)
        keep = min(WINDOW, T)
        new_k, new_v = k[:, T - keep:], v[:, T - keep:]
    else:
        o = _attn_sample(q, jnp.concatenate([past_k.astype(k.dtype), k], 1),
                         jnp.concatenate([past_v.astype(v.dtype), v], 1), sinks)
        new_k, new_v = k, v
    merged = g_rnn * (y_rnn @ w_br_rnn) + g_attn * (o @ w_br_attn)
    x = _layernorm(DN_ALPHA * x + merged @ w_out, ln1_g, ln1_b)
    x2 = x.reshape(B * T, D)
    eidx, gate = _route(x2, w_router, e_bias)
    shared = (jax.nn.silu(x2 @ ws1) * (x2 @ ws3)) @ ws2
    ff = shared + _moe_routed(x2, eidx, gate, w1, w3, w2)
    x = _layernorm(DN_ALPHA * x + ff.reshape(B, T, D), ln2_g, ln2_b)
    return x, new_k, new_v, h_last, new_buf


def setup_inputs(seed: int = 0) -> dict:
    key = jax.random.key(seed)
    ks = iter(jax.random.split(key, 40))

    def nrm(shape, scale):
        return jax.random.normal(next(ks), shape, jnp.float32) * scale

    win_rows = min(WINDOW, PAST_LEN)
    u = jax.random.uniform(next(ks), (DEPTH, D_RNN), jnp.float32, minval=0.81, maxval=0.998)
    a_param = -jnp.log(jnp.expm1(-0.5 * jnp.log(u)))
    return {
        'x_prompt': nrm((BATCH, SEQ, D_MODEL), 1.0),
        'x_sample': nrm((DEC_BATCH, DEC_SEQ, D_MODEL), 1.0),
        'cache_k': nrm((DEPTH, DEC_BATCH, win_rows, N_KV_HEADS, HEAD_DIM), 1.0),
        'cache_v': nrm((DEPTH, DEC_BATCH, win_rows, N_KV_HEADS, HEAD_DIM), 1.0),
        'state_h': nrm((DEPTH, DEC_BATCH, D_RNN), 0.5),
        'state_conv': nrm((DEPTH, DEC_BATCH, CONV_W - 1, D_RNN), 1.0),
        'w_in': nrm((DEPTH, D_MODEL, IN_COLS), D_MODEL ** -0.5),
        'conv_w': nrm((DEPTH, CONV_W, D_RNN), CONV_W ** -0.5),
        'conv_b': nrm((DEPTH, D_RNN), 0.02),
        'w_a': nrm((DEPTH, RNN_BLOCKS, RNN_BW, RNN_BW), RNN_BW ** -0.5),
        'b_a': nrm((DEPTH, D_RNN), 0.02),
        'w_x': nrm((DEPTH, RNN_BLOCKS, RNN_BW, RNN_BW), RNN_BW ** -0.5),
        'b_x': nrm((DEPTH, D_RNN), 0.02),
        'a_param': a_param,
        'sinks': nrm((DEPTH, N_Q_HEADS), 0.5),
        'w_br_rnn': nrm((DEPTH, D_RNN, D_MODEL), D_RNN ** -0.5 * DN_BETA),
        'w_br_attn': nrm((DEPTH, N_Q_HEADS * HEAD_DIM, D_MODEL), (N_Q_HEADS * HEAD_DIM) ** -0.5 * DN_BETA),
        'w_out': nrm((DEPTH, D_MODEL, D_MODEL), D_MODEL ** -0.5 * DN_BETA),
        'ln1_g': 1.0 + nrm((DEPTH, D_MODEL), 0.02),
        'ln1_b': nrm((DEPTH, D_MODEL), 0.02),
        'w_router': nrm((DEPTH, D_MODEL, N_EXPERTS), D_MODEL ** -0.5),
        'e_bias': nrm((DEPTH, N_EXPERTS), 0.01),
        'w1': nrm((DEPTH, N_EXPERTS, D_MODEL, D_EXPERT), D_MODEL ** -0.5),
        'w3': nrm((DEPTH, N_EXPERTS, D_MODEL, D_EXPERT), D_MODEL ** -0.5),
        'w2': nrm((DEPTH, N_EXPERTS, D_EXPERT, D_MODEL), D_EXPERT ** -0.5 * DN_BETA),
        'ws1': nrm((DEPTH, D_MODEL, D_SHARED), D_MODEL ** -0.5),
        'ws3': nrm((DEPTH, D_MODEL, D_SHARED), D_MODEL ** -0.5),
        'ws2': nrm((DEPTH, D_SHARED, D_MODEL), D_SHARED ** -0.5 * DN_BETA),
        'ln2_g': 1.0 + nrm((DEPTH, D_MODEL), 0.02),
        'ln2_b': nrm((DEPTH, D_MODEL), 0.02),
    }


def reference(x_prompt, x_sample, cache_k, cache_v, state_h, state_conv,
              w_in, conv_w, conv_b, w_a, b_a, w_x, b_x, a_param, sinks,
              w_br_rnn, w_br_attn, w_out, ln1_g, ln1_b, w_router, e_bias,
              w1, w3, w2, ws1, ws3, ws2, ln2_g, ln2_b):
    Bp, Tp, _ = x_prompt.shape
    Ts = x_sample.shape[1]
    pos_p = jnp.arange(Tp, dtype=jnp.int32)
    pos_s = PAST_LEN + jnp.arange(Ts, dtype=jnp.int32)
    xp, xs = x_prompt, x_sample
    kp_l, vp_l, hp_l, cp_l = [], [], [], []
    ks_l, vs_l, hs_l, cs_l = [], [], [], []
    for l in range(DEPTH):
        lw = (w_in[l], conv_w[l], conv_b[l], w_a[l], b_a[l], w_x[l], b_x[l], a_param[l], sinks[l],
              w_br_rnn[l], w_br_attn[l], w_out[l], ln1_g[l], ln1_b[l], w_router[l], e_bias[l],
              w1[l], w3[l], w2[l], ws1[l], ws3[l], ws2[l], ln2_g[l], ln2_b[l])
        conv0 = jnp.zeros((Bp, CONV_W - 1, D_RNN), xp.dtype)
        h0 = jnp.zeros((Bp, D_RNN), xp.dtype)
        xp, kp, vp, hp, cp = _layer(xp, pos_p, True, conv0, h0, None, None, lw)
        xs, kn, vn, hn, cn = _layer(xs, pos_s, False, state_conv[l], state_h[l], cache_k[l], cache_v[l], lw)
        kp_l.append(kp); vp_l.append(vp); hp_l.append(hp); cp_l.append(cp)
        ks_l.append(kn); vs_l.append(vn); hs_l.append(hn); cs_l.append(cn)
    return (xp, xs,
            jnp.stack(kp_l), jnp.stack(vp_l), jnp.stack(hp_l), jnp.stack(cp_l),
            jnp.stack(ks_l), jnp.stack(vs_l), jnp.stack(hs_l), jnp.stack(cs_l))
```

```python
import functools

import jax
import jax.numpy as jnp
from jax import lax
from jax.experimental import pallas as pl
from jax.experimental.pallas import tpu as pltpu

F32 = jnp.float32
BF16 = jnp.bfloat16
I32 = jnp.int32

CHUNK = 64
WINDOW = 128
HEAD_DIM = 64
N_KV_HEADS = 2
CONV_W = 4
RG_C = 8.0
ROPE_THETA = 10000.0
PAST_LEN = 2048
N_GROUPS = 8
TOPK_GROUPS = 4
TOP_K = 8
ROUTED_SCALE = 2.5
LN_EPS = 1e-5
RNN_BLOCK_PACK = 4
NEG_BIG = -1e30

LANES = 128
SUBLANES = 8
VMEM_LIMIT = 56 * 1024 * 1024

TOKEN_TILE = 512
RNN_TILE = 256
ATTN_TILE = 2 * CHUNK
MOE_ROWS = 256
COMBINE_TILE = 256


def _cparams(sem):
    return pltpu.CompilerParams(dimension_semantics=sem, vmem_limit_bytes=VMEM_LIMIT)


def _const_spec(shape):
    zeros = (0,) * len(shape)
    return pl.BlockSpec(shape, lambda *_: zeros)


def _sigmoid(x):
    return 1.0 / (1.0 + jnp.exp(-x))


def _layernorm(t, g, b):
    mu = jnp.mean(t, axis=-1, keepdims=True)
    c = t - mu
    var = jnp.mean(c * c, axis=-1, keepdims=True)
    return c * lax.rsqrt(var + LN_EPS) * g + b


def _inproj_kernel(x_ref, w_ref, cos_ref, sin_ref, xr_ref, q_ref, k_ref, v_ref, gr_ref, ga_ref):
    tm, d = x_ref.shape
    kv = k_ref.shape[1]
    xb = x_ref[...].astype(BF16)
    cos = cos_ref[...]
    sin = sin_ref[...]
    lane = lax.broadcasted_iota(I32, (tm, LANES), 1)
    low_half = (lane % HEAD_DIM) < (HEAD_DIM // 2)

    def mm(lo, hi):
        return jnp.dot(xb, w_ref[:, lo:hi], preferred_element_type=F32)

    def rope(t):
        partner = jnp.where(low_half, pltpu.roll(t, LANES - HEAD_DIM // 2, 1),
                            pltpu.roll(t, HEAD_DIM // 2, 1))
        return t * cos + partner * sin

    xr_ref[...] = mm(0, d)
    q_off = d
    k_off = q_off + d
    v_off = k_off + kv
    gr_off = v_off + kv
    ga_off = gr_off + d
    for j in range(d // LANES):
        zq = mm(q_off + j * LANES, q_off + (j + 1) * LANES)
        q_ref[:, j * LANES:(j + 1) * LANES] = rope(zq).astype(BF16)
    for j in range(kv // LANES):
        zk = mm(k_off + j * LANES, k_off + (j + 1) * LANES)
        k_ref[:, j * LANES:(j + 1) * LANES] = rope(zk)
    v_ref[...] = mm(v_off, v_off + kv)
    gr_ref[...] = _sigmoid(mm(gr_off, gr_off + d)).astype(BF16)
    ga_ref[...] = _sigmoid(mm(ga_off, ga_off + d)).astype(BF16)


def _inproj(x, w_in_b, cos_t, sin_t):
    n, d = x.shape
    kv = N_KV_HEADS * HEAD_DIM
    tm = TOKEN_TILE
    row = lambda w: pl.BlockSpec((tm, w), lambda i: (i, 0))
    return pl.pallas_call(
        _inproj_kernel,
        grid=(n // tm,),
        in_specs=[row(d), _const_spec(w_in_b.shape), row(LANES), row(LANES)],
        out_specs=[row(d), row(d), row(kv), row(kv), row(d), row(d)],
        out_shape=[jax.ShapeDtypeStruct((n, d), F32), jax.ShapeDtypeStruct((n, d), BF16),
                   jax.ShapeDtypeStruct((n, kv), F32), jax.ShapeDtypeStruct((n, kv), F32),
                   jax.ShapeDtypeStruct((n, d), BF16), jax.ShapeDtypeStruct((n, d), BF16)],
        compiler_params=_cparams(("parallel",)),
        name="inproj",
    )(x, w_in_b, cos_t, sin_t)


def _rglru_kernel(xr_ref, cst_ref, h0_ref, cw_ref, cb_ref, wx_ref, wa_ref, bx_ref, ba_ref, sp_ref,
                  yin_ref, y_ref, hl_ref, cn_ref, xs_ref, hc_ref, *, first_reset):
    del yin_ref
    tt = pl.program_id(1)
    t_rows, d = xr_ref.shape

    @pl.when(tt == 0)
    def _():
        xs_ref[0:SUBLANES, :] = cst_ref[0]
        hc_ref[...] = h0_ref[0]

    @pl.when(tt > 0)
    def _():
        xs_ref[0:SUBLANES, :] = xs_ref[t_rows:t_rows + SUBLANES, :]

    x = xr_ref[...]
    xs_ref[SUBLANES:, :] = x
    cw = cw_ref[...]
    u = cb_ref[...] + cw[CONV_W - 1:CONV_W, :] * x
    for j in range(CONV_W - 1):
        off = SUBLANES - (CONV_W - 1) + j
        u = u + cw[j:j + 1, :] * xs_ref[off:off + t_rows, :]

    ub = u.astype(BF16)
    pack = wx_ref.shape[1]
    gx_parts, ga_parts = [], []
    for j in range(d // pack):
        seg = ub[:, j * pack:(j + 1) * pack]
        gx_parts.append(jnp.dot(seg, wx_ref[j], preferred_element_type=F32))
        ga_parts.append(jnp.dot(seg, wa_ref[j], preferred_element_type=F32))
    gx = _sigmoid(jnp.concatenate(gx_parts, axis=1) + bx_ref[...])
    ga = _sigmoid(jnp.concatenate(ga_parts, axis=1) + ba_ref[...])
    log_a = -RG_C * ga * sp_ref[...]
    a = jnp.exp(log_a)
    mult = jnp.sqrt(1.0 - a * a)
    row = lax.broadcasted_iota(I32, (t_rows, d), 0)
    if first_reset:
        mult = jnp.where((row == 0) & (tt == 0), 1.0, mult)
    b = mult * gx * u

    s = 1
    while s < t_rows:
        a_prev = pltpu.roll(a, s, 0)
        b_prev = pltpu.roll(b, s, 0)
        keep = row >= s
        b = jnp.where(keep, a * b_prev + b, b)
        a = jnp.where(keep, a * a_prev, a)
        s *= 2
    h = b + a * hc_ref[...]
    y_ref[...] = h.astype(BF16)
    hc_ref[...] = h[t_rows - 1:t_rows, :]

    @pl.when(tt == pl.num_programs(1) - 1)
    def _():
        hl_ref[0] = h[t_rows - 1:t_rows, :]
        cn_ref[0] = x[t_rows - SUBLANES:, :]


def _rglru(xr, y_prev, conv_state8, h0, cw, cb, wx_bd, wa_bd, bx, ba, sp, *,
           row0, batch, seq, tile, first_reset):
    n, d = xr.shape
    steps = seq // tile
    base = row0 // tile
    rows = pl.BlockSpec((tile, d), lambda b, t: (base + b * steps + t, 0))
    per_b = lambda r: pl.BlockSpec((1, r, d), lambda b, t: (b, 0, 0))
    vec = _const_spec((1, d))
    kern = functools.partial(_rglru_kernel, first_reset=first_reset)
    return pl.pallas_call(
        kern,
        grid=(batch, steps),
        in_specs=[rows, per_b(SUBLANES), per_b(1), _const_spec(cw.shape), vec,
                  _const_spec(wx_bd.shape), _const_spec(wa_bd.shape), vec, vec, vec,
                  pl.BlockSpec(memory_space=pl.ANY)],
        out_specs=[rows, per_b(1), per_b(SUBLANES)],
        out_shape=[jax.ShapeDtypeStruct((n, d), BF16),
                   jax.ShapeDtypeStruct((batch, 1, d), F32),
                   jax.ShapeDtypeStruct((batch, SUBLANES, d), F32)],
        scratch_shapes=[pltpu.VMEM((tile + SUBLANES, d), F32), pltpu.VMEM((1, d), F32)],
        input_output_aliases={10: 0},
        compiler_params=_cparams(("parallel", "arbitrary")),
        name="rglru_reset" if first_reset else "rglru_carry",
    )(xr, conv_state8, h0, cw, cb, wx_bd, wa_bd, bx, ba, sp, y_prev)


def _attend(q_ref, o_ref, sinks_ref, k, v, valid):
    tq = q_ref.shape[0]
    s_len = k.shape[0]
    n_q = q_ref.shape[1] // HEAD_DIM
    group = n_q // N_KV_HEADS
    lane = lax.broadcasted_iota(I32, (s_len, LANES), 1)
    nt = (((1,), (1,)), ((), ()))
    scale = HEAD_DIM ** -0.5
    for kh in range(N_KV_HEADS):
        own = (lane // HEAD_DIM) == kh
        k_own = jnp.where(own, k, 0.0)
        v_own = jnp.where(own, v, 0.0)
        k_other = pltpu.roll(k_own, HEAD_DIM, 1)
        v_other = pltpu.roll(v_own, HEAD_DIM, 1)
        k_side = [t.astype(BF16) for t in ((k_own, k_other) if kh == 0 else (k_other, k_own))]
        v_side = [t.astype(BF16) for t in ((v_own, v_other) if kh == 0 else (v_other, v_own))]
        for p in range(group // 2):
            c0 = kh * group * HEAD_DIM + p * LANES
            qp = q_ref[:, c0:c0 + LANES]
            acc = jnp.zeros((tq, LANES), F32)
            for side in range(2):
                head = kh * group + 2 * p + side
                sink = sinks_ref[head]
                sc = lax.dot_general(qp, k_side[side], nt, preferred_element_type=F32) * scale
                sc = jnp.where(valid, sc, NEG_BIG)
                m = jnp.maximum(jnp.max(sc, axis=1, keepdims=True), sink)
                e = jnp.exp(sc - m)
                den = jnp.sum(e, axis=1, keepdims=True) + jnp.exp(sink - m)
                pv = jnp.dot(e.astype(BF16), v_side[side], preferred_element_type=F32)
                acc = acc + pv / den
            o_ref[:, c0:c0 + LANES] = acc.astype(BF16)


def _attn_prompt_kernel(sinks_ref, q_ref, kp_ref, kc_ref, vp_ref, vc_ref, oin_ref, o_ref):
    del oin_ref
    i = pl.program_id(1)
    tq = q_ref.shape[0]
    k = jnp.concatenate([kp_ref[...], kc_ref[...]], axis=0)
    v = jnp.concatenate([vp_ref[...], vc_ref[...]], axis=0)
    qc = lax.broadcasted_iota(I32, (tq, 2 * tq), 0) // CHUNK
    col = lax.broadcasted_iota(I32, (tq, 2 * tq), 1)
    kc = col // CHUNK
    valid = (kc >= qc) & (kc <= qc + WINDOW // CHUNK) & ((col >= tq) | (i > 0))
    _attend(q_ref, o_ref, sinks_ref, k, v, valid)


def _attn_sample_kernel(sinks_ref, q_ref, k_ref, v_ref, oin_ref, o_ref, *, n_valid):
    del oin_ref
    tq = q_ref.shape[0]
    k = k_ref[0]
    v = v_ref[0]
    valid = lax.broadcasted_iota(I32, (tq, k.shape[0]), 1) < n_valid
    _attend(q_ref, o_ref, sinks_ref, k, v, valid)


def _attn_prompt(q, k, v, sinks, o_prev, *, row0, batch, seq):
    n, d = q.shape
    kv = k.shape[1]
    tq = ATTN_TILE
    steps = seq // tq
    base = row0 // tq
    cur = lambda w: pl.BlockSpec((tq, w), lambda b, i: (base + b * steps + i, 0))
    prev = lambda w: pl.BlockSpec((tq, w), lambda b, i: (base + b * steps + jnp.maximum(i - 1, 0), 0))
    return pl.pallas_call(
        _attn_prompt_kernel,
        grid=(batch, steps),
        in_specs=[pl.BlockSpec(memory_space=pltpu.SMEM), cur(d), prev(kv), cur(kv), prev(kv), cur(kv),
                  pl.BlockSpec(memory_space=pl.ANY)],
        out_specs=cur(d),
        out_shape=jax.ShapeDtypeStruct((n, d), BF16),
        input_output_aliases={6: 0},
        compiler_params=_cparams(("parallel", "arbitrary")),
        name="attn_prompt",
    )(sinks, q, k, k, v, v, o_prev)


def _attn_sample(q, k_all, v_all, sinks, o_prev, *, row0, batch, seq, n_valid):
    n, d = q.shape
    s_pad, kv = k_all.shape[1:]
    base = row0 // seq
    rows = pl.BlockSpec((seq, d), lambda b: (base + b, 0))
    keys = pl.BlockSpec((1, s_pad, kv), lambda b: (b, 0, 0))
    return pl.pallas_call(
        functools.partial(_attn_sample_kernel, n_valid=n_valid),
        grid=(batch,),
        in_specs=[pl.BlockSpec(memory_space=pltpu.SMEM), rows, keys, keys,
                  pl.BlockSpec(memory_space=pl.ANY)],
        out_specs=rows,
        out_shape=jax.ShapeDtypeStruct((n, d), BF16),
        input_output_aliases={4: 0},
        compiler_params=_cparams(("parallel",)),
        name="attn_sample",
    )(sinks, q, k_all, v_all, o_prev)


def _mix_kernel(y_ref, o_ref, gr_ref, ga_ref, x_ref, wbr_ref, wba_ref, wo_ref, g_ref, b_ref,
                wrt_ref, eb_ref, tri_ref, elow_ref,
                x1_ref, eidx_ref, gate_ref, rank_ref, cnt_ref, carry_ref, *, alpha):
    i = pl.program_id(0)
    tm = x_ref.shape[0]
    n_e = wrt_ref.shape[0]
    per_g = n_e // N_GROUPS

    @pl.when(i == 0)
    def _():
        carry_ref[...] = jnp.zeros_like(carry_ref)

    r = jnp.dot(y_ref[...], wbr_ref[...], preferred_element_type=F32)
    a = jnp.dot(o_ref[...], wba_ref[...], preferred_element_type=F32)
    merged = gr_ref[...].astype(F32) * r + ga_ref[...].astype(F32) * a
    t = alpha * x_ref[...] + jnp.dot(merged.astype(BF16), wo_ref[...], preferred_element_type=F32)
    x1 = _layernorm(t, g_ref[...], b_ref[...])
    x1_ref[...] = x1

    nt = (((1,), (1,)), ((), ()))
    logits = lax.dot_general(wrt_ref[...], x1.astype(BF16), nt, preferred_element_type=F32)
    scores = _sigmoid(logits)
    biased = scores + eb_ref[...]
    g3 = biased.reshape(N_GROUPS, per_g, tm)
    sub = lax.broadcasted_iota(I32, (N_GROUPS, per_g, tm), 1)
    m1 = jnp.max(g3, axis=1, keepdims=True)
    first = jnp.min(jnp.where(g3 == m1, sub, per_g), axis=1, keepdims=True)
    m2 = jnp.max(jnp.where(sub == first, -jnp.inf, g3), axis=1, keepdims=True)
    gs = m1 + m2
    gi = lax.broadcasted_iota(I32, (N_GROUPS, 1, tm), 0)
    beaten = jnp.zeros((N_GROUPS, 1, tm), F32)
    for g in range(N_GROUPS):
        other = gs[g:g + 1]
        beats = (other > gs) | ((other == gs) & (gi > g))
        beaten = beaten + beats.astype(F32)
    gmask = beaten < TOPK_GROUPS
    masked = jnp.where(gmask, g3, -jnp.inf).reshape(n_e, tm)
    ei = lax.broadcasted_iota(I32, (n_e, tm), 0)
    beaten = jnp.zeros((n_e, tm), F32)
    for e in range(n_e):
        other = masked[e:e + 1, :]
        beats = (other > masked) | ((other == masked) & (ei > e))
        beaten = beaten + beats.astype(F32)
    sel = beaten < TOP_K
    self32 = sel.astype(F32)
    w = jnp.where(sel, scores, 0.0)
    gate_dense = w / jnp.sum(w, axis=0, keepdims=True) * ROUTED_SCALE

    selb = self32.astype(BF16)
    before = jnp.dot(selb, tri_ref[...], preferred_element_type=F32) + carry_ref[...]
    carry_ref[...] = carry_ref[...] + jnp.sum(self32, axis=1, keepdims=True)
    cnt_ref[...] = carry_ref[...]
    slot = jnp.dot(elow_ref[...], selb, preferred_element_type=F32)
    eif = ei.astype(F32)
    e_rows, g_rows, r_rows = [], [], []
    for j in range(TOP_K):
        pick = sel & (slot == float(j))
        e_rows.append(jnp.sum(jnp.where(pick, eif, 0.0), axis=0, keepdims=True))
        g_rows.append(jnp.sum(jnp.where(pick, gate_dense, 0.0), axis=0, keepdims=True))
        r_rows.append(jnp.sum(jnp.where(pick, before, 0.0), axis=0, keepdims=True))
    eidx_ref[...] = jnp.concatenate(e_rows, axis=0).astype(I32)
    gate_ref[...] = jnp.concatenate(g_rows, axis=0)
    rank_ref[...] = jnp.concatenate(r_rows, axis=0).astype(I32)


def _mix(y, o, gr, ga, x, wbr, wba, wo, g1, b1, wrt, eb, tri, elow, *, alpha):
    n, d = x.shape
    n_e = wrt.shape[0]
    tm = TOKEN_TILE
    row = pl.BlockSpec((tm, d), lambda i: (i, 0))
    lane_rows = pl.BlockSpec((TOP_K, tm), lambda i: (0, i))
    return pl.pallas_call(
        functools.partial(_mix_kernel, alpha=alpha),
        grid=(n // tm,),
        in_specs=[row, row, row, row, row, _const_spec(wbr.shape), _const_spec(wba.shape),
                  _const_spec(wo.shape), _const_spec((1, d)), _const_spec((1, d)),
                  _const_spec(wrt.shape), _const_spec(eb.shape), _const_spec(tri.shape),
                  _const_spec(elow.shape)],
        out_specs=[row, lane_rows, lane_rows, lane_rows, _const_spec((n_e, 1))],
        out_shape=[jax.ShapeDtypeStruct((n, d), F32), jax.ShapeDtypeStruct((TOP_K, n), I32),
                   jax.ShapeDtypeStruct((TOP_K, n), F32), jax.ShapeDtypeStruct((TOP_K, n), I32),
                   jax.ShapeDtypeStruct((n_e, 1), F32)],
        scratch_shapes=[pltpu.VMEM((n_e, 1), F32)],
        compiler_params=_cparams(("arbitrary",)),
        name="mix_route",
    )(y, o, gr, ga, x, wbr, wba, wo, g1, b1, wrt, eb, tri, elow)


def _row_gather_start(src_hbm, idx_at, dst, sem, n_rows, unroll):
    def body(r, carry):
        pltpu.make_async_copy(src_hbm.at[pl.ds(idx_at(r), 1), :], dst.at[pl.ds(r, 1), :], sem).start()
        return carry
    lax.fori_loop(0, n_rows, body, 0, unroll=unroll)


def _row_gather_wait(src_hbm, dst, sem):
    pltpu.make_async_copy(src_hbm.at[pl.ds(0, dst.shape[0]), :], dst, sem).wait()


def _moe_kernel(blk_e_ref, nblk_ref, tok_ref, tokn_ref, x_hbm, w1_ref, w3_ref, w2_ref, ys_ref,
                xbuf, w1b, w3b, w2b, sem):
    i = pl.program_id(0)
    used = nblk_ref[0]
    rows = xbuf.shape[1]
    slot = i % 2

    @pl.when(i == 0)
    def _():
        _row_gather_start(x_hbm, lambda r: tok_ref[0, 0, r], xbuf.at[0], sem.at[0], rows, 8)

    @pl.when(i + 1 < used)
    def _():
        _row_gather_start(x_hbm, lambda r: tokn_ref[0, 0, r], xbuf.at[1 - slot], sem.at[1 - slot], rows, 8)

    @pl.when(i < used)
    def _():
        new_expert = (i == 0) | (blk_e_ref[i] != blk_e_ref[jnp.maximum(i - 1, 0)])

        @pl.when(new_expert)
        def _():
            w1b[...] = w1_ref[0].astype(BF16)
            w3b[...] = w3_ref[0].astype(BF16)
            w2b[...] = w2_ref[0].astype(BF16)

        _row_gather_wait(x_hbm, xbuf.at[slot], sem.at[slot])
        xb = xbuf[slot].astype(BF16)
        h1 = jnp.dot(xb, w1b[...], preferred_element_type=F32)
        h3 = jnp.dot(xb, w3b[...], preferred_element_type=F32)
        hid = (h1 * _sigmoid(h1) * h3).astype(BF16)
        ys_ref[...] = jnp.dot(hid, w2b[...], preferred_element_type=F32)

    @pl.when(i >= used)
    def _():
        ys_ref[...] = jnp.zeros_like(ys_ref)


def _moe(blk_e, n_used, ptok3, x1, w1, w3, w2):
    n_blk, _, rows = ptok3.shape
    d = x1.shape[1]
    de = w1.shape[2]
    tok_cur = pl.BlockSpec((1, 1, rows), lambda i, be, nu: (i, 0, 0), memory_space=pltpu.SMEM)
    tok_next = pl.BlockSpec((1, 1, rows), lambda i, be, nu: (jnp.minimum(i + 1, n_blk - 1), 0, 0),
                            memory_space=pltpu.SMEM)
    wspec = lambda shape: pl.BlockSpec((1,) + shape, lambda i, be, nu: (be[i], 0, 0))
    grid_spec = pltpu.PrefetchScalarGridSpec(
        num_scalar_prefetch=2,
        grid=(n_blk,),
        in_specs=[tok_cur, tok_next, pl.BlockSpec(memory_space=pl.ANY),
                  wspec((d, de)), wspec((d, de)), wspec((de, d))],
        out_specs=pl.BlockSpec((rows, d), lambda i, be, nu: (i, 0)),
        scratch_shapes=[pltpu.VMEM((2, rows, d), F32), pltpu.VMEM((d, de), BF16),
                        pltpu.VMEM((d, de), BF16), pltpu.VMEM((de, d), BF16),
                        pltpu.SemaphoreType.DMA((2,))],
    )
    return pl.pallas_call(
        _moe_kernel,
        grid_spec=grid_spec,
        out_shape=jax.ShapeDtypeStruct((n_blk * rows, d), F32),
        compiler_params=_cparams(("arbitrary",)),
        name="moe_experts",
    )(blk_e, n_used, ptok3, ptok3, x1, w1, w3, w2)


def _combine_kernel(dst_ref, dstn_ref, ys_hbm, gate_ref, x1_ref, ws1_ref, ws3_ref, ws2_ref, g_ref, b_ref,
                    out_ref, gbuf, sem, *, alpha):
    i = pl.program_id(0)
    n_steps = pl.num_programs(0)
    tm = x1_ref.shape[0]
    slot = i % 2

    def start(d_ref, s):
        for k in range(TOP_K):
            _row_gather_start(ys_hbm, lambda r, k=k: d_ref[0, k, r], gbuf.at[s, pl.ds(k * tm, tm)],
                              sem.at[s], tm, 8)

    @pl.when(i == 0)
    def _():
        start(dst_ref, 0)

    @pl.when(i + 1 < n_steps)
    def _():
        start(dstn_ref, 1 - slot)

    x1 = x1_ref[...]
    xb = x1.astype(BF16)
    s1 = jnp.dot(xb, ws1_ref[...], preferred_element_type=F32)
    s3 = jnp.dot(xb, ws3_ref[...], preferred_element_type=F32)
    ff = jnp.dot((s1 * _sigmoid(s1) * s3).astype(BF16), ws2_ref[...], preferred_element_type=F32)
    _row_gather_wait(ys_hbm, gbuf.at[slot], sem.at[slot])
    gate = gate_ref[...]
    for k in range(TOP_K):
        ff = ff + gate[:, k:k + 1] * gbuf[slot, pl.ds(k * tm, tm), :]
    out_ref[...] = _layernorm(alpha * x1 + ff, g_ref[...], b_ref[...])


def _combine(dest3, ys, gate, x1, ws1, ws3, ws2, g2, b2, *, alpha):
    n, d = x1.shape
    steps, _, tm = dest3.shape
    cur = pl.BlockSpec((1, TOP_K, tm), lambda i: (i, 0, 0), memory_space=pltpu.SMEM)
    nxt = pl.BlockSpec((1, TOP_K, tm), lambda i: (jnp.minimum(i + 1, steps - 1), 0, 0),
                       memory_space=pltpu.SMEM)
    row = lambda w: pl.BlockSpec((tm, w), lambda i: (i, 0))
    return pl.pallas_call(
        functools.partial(_combine_kernel, alpha=alpha),
        grid=(steps,),
        in_specs=[cur, nxt, pl.BlockSpec(memory_space=pl.ANY), row(TOP_K), row(d),
                  _const_spec(ws1.shape), _const_spec(ws3.shape), _const_spec(ws2.shape),
                  _const_spec((1, d)), _const_spec((1, d))],
        out_specs=row(d),
        out_shape=jax.ShapeDtypeStruct((n, d), F32),
        scratch_shapes=[pltpu.VMEM((2, TOP_K * tm, d), F32), pltpu.SemaphoreType.DMA((2,))],
        compiler_params=_cparams(("arbitrary",)),
        name="moe_combine",
    )(dest3, dest3, ys, gate, x1, ws1, ws3, ws2, g2, b2)


def _rope_tables(pos):
    half = HEAD_DIM // 2
    inv = jnp.power(ROPE_THETA, -jnp.arange(half, dtype=F32) / half)
    ang = pos.astype(F32)[:, None] * inv[None, :]
    cos = jnp.cos(ang)
    sin = jnp.sin(ang)
    reps = LANES // HEAD_DIM
    return (jnp.tile(jnp.concatenate([cos, cos], axis=1), (1, reps)),
            jnp.tile(jnp.concatenate([-sin, sin], axis=1), (1, reps)))


def _block_diag(w, pack):
    nb, c, _ = w.shape
    w4 = w.reshape(nb // pack, pack, c, c)
    eye = jnp.eye(pack, dtype=w.dtype)
    return jnp.einsum('gpcd,pq->gpcqd', w4, eye).reshape(nb // pack, pack * c, pack * c)


def _route_plan(eidx_t, rank_t, counts, n_blk, rows):
    n_e = counts.shape[0]
    n = eidx_t.shape[1]
    pcounts = (counts + rows - 1) // rows * rows
    pend = jnp.cumsum(pcounts)
    pstart = pend - pcounts
    dest = pstart[eidx_t] + rank_t
    tok = jnp.broadcast_to(jnp.arange(n, dtype=I32)[None, :], dest.shape)
    ptok = jnp.zeros((n_blk * rows,), I32).at[dest.reshape(-1)].set(tok.reshape(-1))
    blk_start = jnp.arange(n_blk, dtype=I32) * rows
    blk_e = jnp.minimum(jnp.sum(pend[None, :] <= blk_start[:, None], axis=1), n_e - 1).astype(I32)
    n_used = (pend[-1] // rows).astype(I32).reshape(1)
    return dest, ptok.reshape(n_blk, 1, rows), blk_e, n_used


def kernel(x_prompt, x_sample, cache_k, cache_v, state_h, state_conv, w_in, conv_w, conv_b, w_a, b_a,
           w_x, b_x, a_param, sinks, w_br_rnn, w_br_attn, w_out, ln1_g, ln1_b, w_router, e_bias,
           w1, w3, w2, ws1, ws3, ws2, ln2_g, ln2_b):
    bp, tp, d = x_prompt.shape
    bs, ts, _ = x_sample.shape
    depth = w_in.shape[0]
    n_e = w_router.shape[2]
    kv = N_KV_HEADS * HEAD_DIM
    n_p, n_s = bp * tp, bs * ts
    n = n_p + n_s
    alpha = (2 * depth) ** 0.25
    past = cache_k.shape[2]
    assert n % TOKEN_TILE == 0 and n % COMBINE_TILE == 0 and n % LANES == 0
    assert tp % RNN_TILE == 0 and tp % ATTN_TILE == 0 and n_p % ts == 0 and ts % SUBLANES == 0

    x = jnp.concatenate([x_prompt.reshape(n_p, d), x_sample.reshape(n_s, d)], axis=0)
    pos = jnp.concatenate([jnp.tile(jnp.arange(tp, dtype=I32), bp),
                           jnp.tile(PAST_LEN + jnp.arange(ts, dtype=I32), bs)])
    cos_t, sin_t = _rope_tables(pos)
    tri = jnp.triu(jnp.ones((TOKEN_TILE, TOKEN_TILE), F32), k=1).astype(BF16)
    elow = jnp.tril(jnp.ones((n_e, n_e), F32), k=-1).astype(BF16)
    n_blk = -(-(n * TOP_K) // MOE_ROWS) + n_e
    s_pad = -(-(past + ts) // LANES) * LANES
    zeros_conv = jnp.zeros((bp, SUBLANES, d), F32)
    zeros_h = jnp.zeros((bp, 1, d), F32)

    outs = [[] for _ in range(8)]
    for l in range(depth):
        xr, q, k, v, gr, ga = _inproj(x, w_in[l].astype(BF16), cos_t, sin_t)

        rnn_w = (conv_w[l], conv_b[l][None, :],
                 _block_diag(w_x[l], RNN_BLOCK_PACK).astype(BF16),
                 _block_diag(w_a[l], RNN_BLOCK_PACK).astype(BF16),
                 b_x[l][None, :], b_a[l][None, :], jax.nn.softplus(-a_param[l])[None, :])
        y0 = jnp.zeros((n, d), BF16)
        y, h_p, c_p = _rglru(xr, y0, zeros_conv, zeros_h, *rnn_w, row0=0, batch=bp, seq=tp,
                             tile=RNN_TILE, first_reset=True)
        conv_s8 = jnp.pad(state_conv[l], ((0, 0), (SUBLANES - (CONV_W - 1), 0), (0, 0)))
        y, h_s, c_s = _rglru(xr, y, conv_s8, state_h[l][:, None, :], *rnn_w, row0=n_p, batch=bs, seq=ts,
                             tile=ts, first_reset=False)

        o0 = jnp.zeros((n, d), BF16)
        o = _attn_prompt(q, k, v, sinks[l], o0, row0=0, batch=bp, seq=tp)
        k_s = k[n_p:].reshape(bs, ts, kv)
        v_s = v[n_p:].reshape(bs, ts, kv)
        pad = ((0, 0), (0, s_pad - past - ts), (0, 0))
        k_all = jnp.pad(jnp.concatenate([cache_k[l].reshape(bs, past, kv), k_s], axis=1), pad)
        v_all = jnp.pad(jnp.concatenate([cache_v[l].reshape(bs, past, kv), v_s], axis=1), pad)
        o = _attn_sample(q, k_all, v_all, sinks[l], o, row0=n_p, batch=bs, seq=ts, n_valid=past + ts)

        x1, eidx_t, gate_t, rank_t, cnt = _mix(
            y, o, gr, ga, x, w_br_rnn[l].astype(BF16), w_br_attn[l].astype(BF16), w_out[l].astype(BF16),
            ln1_g[l][None, :], ln1_b[l][None, :], w_router[l].T.astype(BF16), e_bias[l][:, None],
            tri, elow, alpha=alpha)

        dest, ptok3, blk_e, n_used = _route_plan(eidx_t, rank_t, cnt[:, 0].astype(I32), n_blk, MOE_ROWS)
        ys = _moe(blk_e, n_used, ptok3, x1, w1[l], w3[l], w2[l])
        dest3 = dest.reshape(TOP_K, n // COMBINE_TILE, COMBINE_TILE).transpose(1, 0, 2)
        x = _combine(dest3, ys, gate_t.T, x1, ws1[l].astype(BF16), ws3[l].astype(BF16),
                     ws2[l].astype(BF16), ln2_g[l][None, :], ln2_b[l][None, :], alpha=alpha)

        keep = min(WINDOW, tp)
        k_p = k[:n_p].reshape(bp, tp, N_KV_HEADS, HEAD_DIM)[:, tp - keep:]
        v_p = v[:n_p].reshape(bp, tp, N_KV_HEADS, HEAD_DIM)[:, tp - keep:]
        for lst, val in zip(outs, (k_p, v_p, h_p[:, 0], c_p[:, SUBLANES - (CONV_W - 1):],
                                   k_s.reshape(bs, ts, N_KV_HEADS, HEAD_DIM),
                                   v_s.reshape(bs, ts, N_KV_HEADS, HEAD_DIM),
                                   h_s[:, 0], c_s[:, SUBLANES - (CONV_W - 1):])):
            lst.append(val)

    return (x[:n_p].reshape(bp, tp, d), x[n_p:].reshape(bs, ts, d)) + tuple(jnp.stack(o) for o in outs)
```
